```python
import jax, jax.numpy as jnp
from jax import lax
import numpy as np

D_MODEL = 1024
BATCH = 4
SEQ = 8192
DEPTH = 4
DEC_BATCH = 2
DEC_SEQ = 8192
PAST_LEN = 128

QK_NOPE_DIM = 128
QK_ROPE_DIM = 64
V_HEAD_DIM = 128
N_HEADS = D_MODEL // V_HEAD_DIM
ATTN_WIDTH = N_HEADS * V_HEAD_DIM
Q_LORA_RANK = 256
KV_LORA_RANK = 128
ROPE_THETA = 10000.0
Q_BLOCK = 128
POOL_WINDOWS = (2, 4, 8, 16)
N_POOL_GROUPS = 4
POOL_WIDTH = D_MODEL
POOL_GROUP = POOL_WIDTH // N_POOL_GROUPS
IN_WIDTH = Q_LORA_RANK + KV_LORA_RANK + QK_ROPE_DIM + POOL_WIDTH + ATTN_WIDTH + POOL_WIDTH
D_FF = 2816
N_MOD = 9
EPS = 1e-6

kernel_name = 'hybrid_mla_pool_macaron_encoder'


def rmsnorm(x, g):
    x32 = x.astype(jnp.float32)
    y = x32 * lax.rsqrt(jnp.mean(x32 * x32, axis=-1, keepdims=True) + EPS) * g.astype(jnp.float32)
    return y.astype(x.dtype)


def modulate(h, shift, scale):
    return h * (1 + scale) + shift


def swiglu(h, w_gu, w_down):
    gu = h @ w_gu
    gate, up = gu[..., :D_FF], gu[..., D_FF:]
    return (jax.nn.silu(gate) * up) @ w_down


def rope_tables(seq_len, dtype):
    inv = 1.0 / (ROPE_THETA ** (jnp.arange(0, QK_ROPE_DIM, 2, dtype=jnp.float32) / QK_ROPE_DIM))
    ang = jnp.arange(seq_len, dtype=jnp.float32)[:, None] * inv[None, :]
    ang = jnp.concatenate([ang, ang], axis=-1)
    return jnp.cos(ang).astype(dtype), jnp.sin(ang).astype(dtype)


def apply_rope(x, cos, sin):
    x1, x2 = x[..., :QK_ROPE_DIM // 2], x[..., QK_ROPE_DIM // 2:]
    return x * cos + jnp.concatenate([-x2, x1], axis=-1) * sin


def latent_attention(q_nope, q_rope, k_nope, k_rope, v):
    B, S, H, _ = q_nope.shape
    nb = S // Q_BLOCK
    scale = (QK_NOPE_DIM + QK_ROPE_DIM) ** -0.5

    def to_blocks(t):
        return jnp.moveaxis(t.reshape(B, nb, Q_BLOCK, *t.shape[2:]), 1, 0)

    def attend(blk):
        qn, qr = blk
        s = (jnp.einsum('bqhd,bkhd->bhqk', qn, k_nope).astype(jnp.float32)
             + jnp.einsum('bqhd,bkd->bhqk', qr, k_rope).astype(jnp.float32))
        p = jax.nn.softmax(s * scale, axis=-1).astype(v.dtype)
        return jnp.einsum('bhqk,bkhd->bqhd', p, v)

    o = lax.map(attend, (to_blocks(q_nope), to_blocks(q_rope)))
    return jnp.moveaxis(o, 0, 1).reshape(B, S, H * V_HEAD_DIM)


def multiscale_pool(u, w_pool, pool_scale):
    B, S, _ = u.shape
    u32 = u.astype(jnp.float32).reshape(B, S, N_POOL_GROUPS, POOL_GROUP)
    cs = jnp.concatenate([jnp.zeros((B, 1, N_POOL_GROUPS, POOL_GROUP), jnp.float32),
                          jnp.cumsum(u32, axis=1)], axis=1)
    half = jnp.array(POOL_WINDOWS, dtype=jnp.int32) // 2
    t = jnp.arange(S, dtype=jnp.int32)[:, None]
    lo = jnp.clip(t - half[None, :], 0, S)
    hi = jnp.clip(t + half[None, :], 0, S)
    g = jnp.arange(N_POOL_GROUPS, dtype=jnp.int32)[None, :]
    win_sum = cs[:, hi, g, :] - cs[:, lo, g, :]
    mean = win_sum / (hi - lo).astype(jnp.float32)[None, :, :, None]
    pooled = (mean - u32).astype(u.dtype)
    mixed = jnp.einsum('bsgc,gcd->bsgd', pooled, w_pool).reshape(B, S, POOL_WIDTH)
    return mixed * pool_scale


def hybrid_mixer(h, cos, sin, w_in, q_a_norm, w_qb, kv_a_norm, w_kvb, w_pool, pool_scale, w_out):
    B, S, _ = h.shape
    z = h @ w_in
    o1 = Q_LORA_RANK
    o2 = o1 + KV_LORA_RANK
    o3 = o2 + QK_ROPE_DIM
    o4 = o3 + POOL_WIDTH
    q_a, kv_a, k_rope, u, gate_logits = z[..., :o1], z[..., o1:o2], z[..., o2:o3], z[..., o3:o4], z[..., o4:]
    q = (rmsnorm(q_a, q_a_norm) @ w_qb).reshape(B, S, N_HEADS, QK_NOPE_DIM + QK_ROPE_DIM)
    q_nope, q_rope = q[..., :QK_NOPE_DIM], q[..., QK_NOPE_DIM:]
    kv = (rmsnorm(kv_a, kv_a_norm) @ w_kvb).reshape(B, S, N_HEADS, QK_NOPE_DIM + V_HEAD_DIM)
    k_nope, v = kv[..., :QK_NOPE_DIM], kv[..., QK_NOPE_DIM:]
    q_rope = apply_rope(q_rope, cos[:, None, :], sin[:, None, :])
    k_rope = apply_rope(k_rope, cos, sin)
    o_attn = latent_attention(q_nope, q_rope, k_nope, k_rope, v)
    o_pool = multiscale_pool(u, w_pool, pool_scale)
    gates = jax.nn.sigmoid(gate_logits)
    g_attn, g_pool = gates[..., :ATTN_WIDTH], gates[..., ATTN_WIDTH:]
    return (g_attn * o_attn + g_pool * o_pool) @ w_out


def encoder_trunk(x, c, w_ada, b_ada, n1_pre, w1_gu, w1_down, n1_post,
                  nm_pre, w_in, q_a_norm, w_qb, kv_a_norm, w_kvb, w_pool, pool_scale, w_out, nm_post,
                  n2_pre, w2_gu, w2_down, n2_post):
    B, S, D = x.shape
    cos, sin = rope_tables(S, x.dtype)
    for l in range(DEPTH):
        mod = (jax.nn.silu(c) @ w_ada[l] + b_ada[l]).reshape(B, N_MOD, D)[:, :, None, :]
        h = modulate(rmsnorm(x, n1_pre[l]), mod[:, 0], mod[:, 1])
        x = x + 0.5 * mod[:, 2] * rmsnorm(swiglu(h, w1_gu[l], w1_down[l]), n1_post[l])
        h = modulate(rmsnorm(x, nm_pre[l]), mod[:, 3], mod[:, 4])
        m = hybrid_mixer(h, cos, sin, w_in[l], q_a_norm[l], w_qb[l], kv_a_norm[l], w_kvb[l],
                         w_pool[l], pool_scale[l], w_out[l])
        x = x + mod[:, 5] * rmsnorm(m, nm_post[l])
        h = modulate(rmsnorm(x, n2_pre[l]), mod[:, 6], mod[:, 7])
        x = x + 0.5 * mod[:, 8] * rmsnorm(swiglu(h, w2_gu[l], w2_down[l]), n2_post[l])
    return x


def setup_inputs(seed: int = 0) -> dict:
    key = jax.random.key(seed)
    ks = jax.random.split(key, 32)
    f32 = jnp.float32
    L, D = DEPTH, D_MODEL

    def nrm(k, shape, scale):
        return jax.random.normal(k, shape, f32) * scale

    def gain(k, shape):
        return 1.0 + 0.02 * jax.random.normal(k, shape, f32)

    return {
        'x_prompt': nrm(ks[0], (BATCH, SEQ, D), 1.0),
        'x_sample': nrm(ks[1], (DEC_BATCH, DEC_SEQ, D), 1.0),
        'c_prompt': nrm(ks[2], (BATCH, D), 1.0),
        'c_sample': nrm(ks[3], (DEC_BATCH, D), 1.0),
        'w_ada': nrm(ks[4], (L, D, N_MOD * D), 0.3 * D ** -0.5),
        'b_ada': nrm(ks[5], (L, N_MOD * D), 0.02),
        'n1_pre': gain(ks[6], (L, D)),
        'w1_gu': nrm(ks[7], (L, D, 2 * D_FF), D ** -0.5),
        'w1_down': nrm(ks[8], (L, D_FF, D), D_FF ** -0.5),
        'n1_post': gain(ks[9], (L, D)),
        'nm_pre': gain(ks[10], (L, D)),
        'w_in': nrm(ks[11], (L, D, IN_WIDTH), D ** -0.5),
        'q_a_norm': gain(ks[12], (L, Q_LORA_RANK)),
        'w_qb': nrm(ks[13], (L, Q_LORA_RANK, N_HEADS * (QK_NOPE_DIM + QK_ROPE_DIM)), Q_LORA_RANK ** -0.5),
        'kv_a_norm': gain(ks[14], (L, KV_LORA_RANK)),
        'w_kvb': nrm(ks[15], (L, KV_LORA_RANK, N_HEADS * (QK_NOPE_DIM + V_HEAD_DIM)), KV_LORA_RANK ** -0.5),
        'w_pool': nrm(ks[16], (L, N_POOL_GROUPS, POOL_GROUP, POOL_GROUP), POOL_GROUP ** -0.5),
        'pool_scale': 1.0 + 0.05 * jax.random.normal(ks[17], (L, POOL_WIDTH), f32),
        'w_out': nrm(ks[18], (L, D, D), D ** -0.5),
        'nm_post': gain(ks[19], (L, D)),
        'n2_pre': gain(ks[20], (L, D)),
        'w2_gu': nrm(ks[21], (L, D, 2 * D_FF), D ** -0.5),
        'w2_down': nrm(ks[22], (L, D_FF, D), D_FF ** -0.5),
        'n2_post': gain(ks[23], (L, D)),
    }


def reference(x_prompt, x_sample, c_prompt, c_sample, w_ada, b_ada, n1_pre, w1_gu, w1_down, n1_post,
              nm_pre, w_in, q_a_norm, w_qb, kv_a_norm, w_kvb, w_pool, pool_scale, w_out, nm_post,
              n2_pre, w2_gu, w2_down, n2_post):
    y_prompt = encoder_trunk(x_prompt, c_prompt, w_ada, b_ada, n1_pre, w1_gu, w1_down, n1_post,
                             nm_pre, w_in, q_a_norm, w_qb, kv_a_norm, w_kvb, w_pool, pool_scale, w_out, nm_post,
                             n2_pre, w2_gu, w2_down, n2_post)
    y_sample = encoder_trunk(x_sample, c_sample, w_ada, b_ada, n1_pre, w1_gu, w1_down, n1_post,
                             nm_pre, w_in, q_a_norm, w_qb, kv_a_norm, w_kvb, w_pool, pool_scale, w_out, nm_post,
                             n2_pre, w2_gu, w2_down, n2_post)
    return (y_prompt, y_sample)
```

```python
import functools
import math

import jax
import jax.numpy as jnp
from jax import lax
from jax.experimental import pallas as pl
from jax.experimental.pallas import tpu as pltpu

N_HEADS = 8
QK_NOPE_DIM = 128
QK_ROPE_DIM = 64
V_HEAD_DIM = 128
Q_LORA_RANK = 256
KV_LORA_RANK = 128
POOL_WINDOWS = (2, 4, 8, 16)
ROPE_THETA = 10000.0
N_MOD = 9
EPS = 1e-6

QK_WIDTH = QK_NOPE_DIM + 2 * QK_ROPE_DIM
POOL_HALO = max(POOL_WINDOWS) // 2

V7X_VMEM_BYTES = 64 * 1024 * 1024
V7X_VMEM_REQUEST_CAP = 56 * 1024 * 1024
SUBLANES = 8
LANES = 128

TOKEN_TILE = 512
Q_TILE = 512
KV_CHUNK = 1024
FFN_CHUNKS = 2

F32 = jnp.float32
BF16 = jnp.bfloat16


def _vmem_limit(estimate_bytes):
    return int(min(max(estimate_bytes, 16 * 1024 * 1024), V7X_VMEM_REQUEST_CAP))


def _resident(shape):
    zeros = (0,) * len(shape)
    return pl.BlockSpec(shape, lambda *_: zeros, pipeline_mode=pl.Buffered(1))


def _rmsnorm(x, g):
    return x * lax.rsqrt(jnp.mean(x * x, axis=-1, keepdims=True) + EPS) * g


def _sigmoid(x):
    return 1.0 / (1.0 + jnp.exp(-x))


def _dot(a, b):
    return jnp.dot(a, b, preferred_element_type=F32)


def _ada_kernel(c_ref, w_ref, b_ref, o_ref):
    c = c_ref[...]
    a = c * _sigmoid(c)
    w = w_ref[...]
    a_hi = a.astype(BF16)
    a_lo = (a - a_hi.astype(F32)).astype(BF16)
    w_hi = w.astype(BF16)
    w_lo = (w - w_hi.astype(F32)).astype(BF16)
    o_ref[...] = _dot(a_hi, w_hi) + (_dot(a_lo, w_hi) + _dot(a_hi, w_lo)) + b_ref[...]


def _ada_modulation(c, w_ada, b_ada):
    n_layers, d, _ = w_ada.shape
    bp = c.shape[0]
    return pl.pallas_call(
        _ada_kernel,
        out_shape=jax.ShapeDtypeStruct((n_layers, N_MOD, bp, d), F32),
        grid=(n_layers, N_MOD),
        in_specs=[
            pl.BlockSpec((bp, d), lambda l, j: (0, 0)),
            pl.BlockSpec((None, d, d), lambda l, j: (l, 0, j)),
            pl.BlockSpec((None, 1, d), lambda l, j: (l, 0, j)),
        ],
        out_specs=pl.BlockSpec((None, None, bp, d), lambda l, j: (l, j, 0, 0)),
        compiler_params=pltpu.CompilerParams(
            dimension_semantics=("arbitrary", "arbitrary"),
            vmem_limit_bytes=_vmem_limit(4 * d * d * 4),
        ),
        name="ada_modulation",
    )(c, w_ada, b_ada.reshape(n_layers, 1, N_MOD * d))


def _ffn_kernel(x_ref, mod_ref, npre_ref, wgu_ref, wd_ref, npost_ref, o_ref, *, row0, d_ff):
    x = x_ref[...]
    shift = mod_ref[row0:row0 + 1, :]
    scale = mod_ref[row0 + 1:row0 + 2, :]
    gate = mod_ref[row0 + 2:row0 + 3, :]
    h = (_rmsnorm(x, npre_ref[...]) * (1.0 + scale) + shift).astype(BF16)
    fc = d_ff // FFN_CHUNKS
    acc = None
    for c in range(FFN_CHUNKS):
        g = _dot(h, wgu_ref[:, c * fc:(c + 1) * fc])
        u = _dot(h, wgu_ref[:, d_ff + c * fc:d_ff + (c + 1) * fc])
        a = ((g * _sigmoid(g)) * u).astype(BF16)
        part = _dot(a, wd_ref[c * fc:(c + 1) * fc, :])
        acc = part if acc is None else acc + part
    o_ref[...] = x + (0.5 * gate) * _rmsnorm(acc, npost_ref[...])


def _ffn(x, mod, n_pre, w_gu, w_down, n_post, *, row0, tiles_per_seq):
    n_tok, d = x.shape
    d_ff = w_down.shape[0]
    tm = TOKEN_TILE
    est = (w_gu.size + w_down.size) * 2 + 4 * tm * d * 4 + 3 * tm * (d_ff // FFN_CHUNKS) * 4 * 2
    return pl.pallas_call(
        functools.partial(_ffn_kernel, row0=row0, d_ff=d_ff),
        out_shape=jax.ShapeDtypeStruct((n_tok, d), F32),
        grid=(n_tok // tm,),
        in_specs=[
            pl.BlockSpec((tm, d), lambda i: (i, 0)),
            pl.BlockSpec((None, N_MOD, d), lambda i: (i // tiles_per_seq, 0, 0)),
            _resident((1, d)),
            _resident(w_gu.shape),
            _resident(w_down.shape),
            _resident((1, d)),
        ],
        out_specs=pl.BlockSpec((tm, d), lambda i: (i, 0)),
        compiler_params=pltpu.CompilerParams(
            dimension_semantics=("arbitrary",),
            vmem_limit_bytes=_vmem_limit(est),
        ),
        name="ffn_half_step",
    )(x, mod, n_pre, w_gu, w_down, n_post)


def _rope(x, cos, sin_signed, first_half):
    rot = jnp.where(first_half, pltpu.roll(x, LANES - QK_ROPE_DIM // 2, axis=1),
                    pltpu.roll(x, QK_ROPE_DIM // 2, axis=1))
    return x * cos + rot * sin_signed


def _qkv_kernel(x_ref, mod_ref, npre_ref, wa_ref, qn_ref, wqb_ref, kvn_ref, wk_ref, wvt_ref,
                cos_ref, sin_ref, q_ref, k_ref, vt_ref, *, q_scale):
    x = x_ref[...]
    shift = mod_ref[3:4, :]
    scale = mod_ref[4:5, :]
    h = (_rmsnorm(x, npre_ref[...]) * (1.0 + scale) + shift).astype(BF16)
    za = _dot(h, wa_ref[...])
    q_a = za[:, :Q_LORA_RANK]
    kv_a = za[:, Q_LORA_RANK:Q_LORA_RANK + KV_LORA_RANK]
    kr = za[:, Q_LORA_RANK + KV_LORA_RANK:]

    cos = cos_ref[...]
    sin_signed = sin_ref[...]
    lane = lax.broadcasted_iota(jnp.int32, (1, LANES), 1)
    first_half = (lane % QK_ROPE_DIM) < (QK_ROPE_DIM // 2)
    slot_a = lane < QK_ROPE_DIM

    qn = _rmsnorm(q_a, qn_ref[...]).astype(BF16)
    q = _dot(qn, wqb_ref[...]) * q_scale
    kvn = _rmsnorm(kv_a, kvn_ref[...]).astype(BF16)
    k_nope = _dot(kvn, wk_ref[...])
    v_t = lax.dot_general(wvt_ref[...], kvn, (((1,), (1,)), ((), ())),
                          preferred_element_type=F32)
    kr = _rope(kr, cos, sin_signed, first_half).astype(BF16)

    nope_all = N_HEADS * QK_NOPE_DIM
    for pair in range(N_HEADS // 2):
        qr = _rope(q[:, nope_all + pair * LANES:nope_all + (pair + 1) * LANES], cos, sin_signed, first_half)
        for sub in range(2):
            hd = 2 * pair + sub
            q_ref[hd, :, :QK_NOPE_DIM] = q[:, hd * QK_NOPE_DIM:(hd + 1) * QK_NOPE_DIM].astype(BF16)
            keep = slot_a if sub == 0 else jnp.logical_not(slot_a)
            q_ref[hd, :, QK_NOPE_DIM:] = jnp.where(keep, qr, 0.0).astype(BF16)
    for hd in range(N_HEADS):
        k_ref[hd, :, :QK_NOPE_DIM] = k_nope[:, hd * QK_NOPE_DIM:(hd + 1) * QK_NOPE_DIM].astype(BF16)
        k_ref[hd, :, QK_NOPE_DIM:] = kr
        vt_ref[hd] = v_t[hd * V_HEAD_DIM:(hd + 1) * V_HEAD_DIM, :].astype(BF16)


def _qkv(x, mod, n_pre, w_a, q_norm, w_qb, kv_norm, w_k, w_vt, cos2, sin2, *, n_seq, seq_len):
    n_tok, d = x.shape
    tm = TOKEN_TILE
    tps = seq_len // tm
    q_scale = (QK_NOPE_DIM + QK_ROPE_DIM) ** -0.5
    weights = (w_a, w_qb, w_k, w_vt)
    est = sum(w.size for w in weights) * 2 + 2 * tm * d * 4 + 4 * 3 * N_HEADS * tm * QK_WIDTH * 2 + 8 * tm * 2048 * 4
    return pl.pallas_call(
        functools.partial(_qkv_kernel, q_scale=q_scale),
        out_shape=(
            jax.ShapeDtypeStruct((n_seq, N_HEADS, seq_len, QK_WIDTH), BF16),
            jax.ShapeDtypeStruct((n_seq, N_HEADS, seq_len, QK_WIDTH), BF16),
            jax.ShapeDtypeStruct((n_seq, N_HEADS, V_HEAD_DIM, seq_len), BF16),
        ),
        grid=(n_tok // tm,),
        in_specs=[
            pl.BlockSpec((tm, d), lambda i: (i, 0)),
            pl.BlockSpec((None, N_MOD, d), lambda i: (i // tps, 0, 0)),
            _resident((1, d)),
            _resident(w_a.shape),
            _resident((1, Q_LORA_RANK)),
            _resident(w_qb.shape),
            _resident((1, KV_LORA_RANK)),
            _resident(w_k.shape),
            _resident(w_vt.shape),
            pl.BlockSpec((tm, LANES), lambda i: (i % tps, 0)),
            pl.BlockSpec((tm, LANES), lambda i: (i % tps, 0)),
        ],
        out_specs=(
            pl.BlockSpec((None, N_HEADS, tm, QK_WIDTH), lambda i: (i // tps, 0, i % tps, 0)),
            pl.BlockSpec((None, N_HEADS, tm, QK_WIDTH), lambda i: (i // tps, 0, i % tps, 0)),
            pl.BlockSpec((None, N_HEADS, V_HEAD_DIM, tm), lambda i: (i // tps, 0, 0, i % tps)),
        ),
        compiler_params=pltpu.CompilerParams(
            dimension_semantics=("arbitrary",),
            vmem_limit_bytes=_vmem_limit(est),
        ),
        name="mixer_qkv",
    )(x, mod, n_pre, w_a, q_norm, w_qb, kv_norm, w_k, w_vt, cos2, sin2)


def _attn_kernel(q_ref, k_ref, vt_ref, o_ref, *, seq_len):
    q = q_ref[...]
    tq = q.shape[0]

    def step(j, carry):
        m, l, acc = carry
        start = pl.multiple_of(j * KV_CHUNK, KV_CHUNK)
        k = k_ref[pl.ds(start, KV_CHUNK), :]
        s_t = lax.dot_general(k, q, (((1,), (1,)), ((), ())), preferred_element_type=F32)
        m_new = jnp.maximum(m, jnp.max(s_t, axis=0, keepdims=True))
        alpha = jnp.exp(m - m_new)
        p_t = jnp.exp(s_t - m_new)
        l_new = alpha * l + jnp.sum(p_t, axis=0, keepdims=True)
        v_t = vt_ref[:, pl.ds(start, KV_CHUNK)]
        acc_new = alpha * acc + _dot(v_t, p_t.astype(BF16))
        return m_new, l_new, acc_new

    init = (jnp.full((1, tq), -jnp.inf, F32), jnp.zeros((1, tq), F32), jnp.zeros((V_HEAD_DIM, tq), F32))
    _, l, acc = lax.fori_loop(0, seq_len // KV_CHUNK, step, init)
    o_ref[...] = (acc / l).T


def _attention(q, k, v_t):
    n_seq, n_heads, seq_len, _ = q.shape
    tq = Q_TILE
    est = 2 * (seq_len * QK_WIDTH * 2 + seq_len * V_HEAD_DIM * 2) + 6 * KV_CHUNK * tq * 4
    return pl.pallas_call(
        functools.partial(_attn_kernel, seq_len=seq_len),
        out_shape=jax.ShapeDtypeStruct((n_seq, seq_len, n_heads * V_HEAD_DIM), F32),
        grid=(n_seq, n_heads, seq_len // tq),
        in_specs=[
            pl.BlockSpec((None, None, tq, QK_WIDTH), lambda b, h, i: (b, h, i, 0)),
            pl.BlockSpec((None, None, seq_len, QK_WIDTH), lambda b, h, i: (b, h, 0, 0)),
            pl.BlockSpec((None, None, V_HEAD_DIM, seq_len), lambda b, h, i: (b, h, 0, 0)),
        ],
        out_specs=pl.BlockSpec((None, tq, V_HEAD_DIM), lambda b, h, i: (b, i, h)),
        compiler_params=pltpu.CompilerParams(
            dimension_semantics=("arbitrary", "arbitrary", "arbitrary"),
            vmem_limit_bytes=_vmem_limit(est),
        ),
        name="latent_attention",
    )(q, k, v_t)


def _post_kernel(x_ref, xprev_ref, xnext_ref, oattn_ref, mod_ref, npre_ref, wu_ref, wg_ref, wpool_ref,
                 pscale_ref, wout_ref, npost_ref, o_ref, e_scr, *, tiles_per_seq, seq_len):
    tm, d = x_ref.shape
    halo = POOL_HALO
    j = pl.program_id(0) % tiles_per_seq
    shift = mod_ref[3:4, :]
    scale = mod_ref[4:5, :]
    gate = mod_ref[5:6, :]
    npre = npre_ref[...]

    def mod_norm(v):
        return _rmsnorm(v, npre) * (1.0 + scale) + shift

    x = x_ref[...]
    h_ext = jnp.concatenate([mod_norm(xprev_ref[...]), mod_norm(x), mod_norm(xnext_ref[...])], axis=0).astype(BF16)
    u_ext = _dot(h_ext, wu_ref[...])
    row = lax.broadcasted_iota(jnp.int32, (tm + 2 * halo, 1), 0)
    outside = jnp.logical_or(jnp.logical_and(j == 0, row < halo),
                             jnp.logical_and(j == tiles_per_seq - 1, row >= tm + halo))
    e_scr[...] = jnp.where(outside, 0.0, u_ext)

    pool_w = wu_ref.shape[1]
    group = pool_w // len(POOL_WINDOWS)
    t = j * tm + lax.broadcasted_iota(jnp.int32, (tm, 1), 0)
    mixed = []
    for g, window in enumerate(POOL_WINDOWS):
        hw = window // 2
        cols = slice(g * group, (g + 1) * group)
        win = e_scr[pl.ds(halo - hw, tm), cols]
        for off in range(halo - hw + 1, halo + hw):
            win = win + e_scr[pl.ds(off, tm), cols]
        count = (jnp.minimum(t + hw, seq_len) - jnp.maximum(t - hw, 0)).astype(F32)
        pooled = (win / count - e_scr[pl.ds(halo, tm), cols]).astype(BF16)
        mixed.append(_dot(pooled, wpool_ref[g]))
    o_pool = jnp.concatenate(mixed, axis=1) * pscale_ref[...]

    h = h_ext[halo:halo + tm, :]
    gates = _sigmoid(_dot(h, wg_ref[...]))
    attn_w = oattn_ref.shape[1]
    merged = (gates[:, :attn_w] * oattn_ref[...] + gates[:, attn_w:] * o_pool).astype(BF16)
    m = _dot(merged, wout_ref[...])
    o_ref[...] = x + gate * _rmsnorm(m, npost_ref[...])


def _post(x, o_attn, mod, n_pre, w_u, w_gate, w_pool, pool_scale, w_out, n_post, *, seq_len):
    n_tok, d = x.shape
    tm = TOKEN_TILE
    tps = seq_len // tm
    halo = POOL_HALO
    hb = tm // halo
    n_hblocks = n_tok // halo
    weights = (w_u, w_gate, w_pool, w_out)
    est = sum(w.size for w in weights) * 2 + 6 * tm * d * 4 + 10 * tm * 2048 * 4
    return pl.pallas_call(
        functools.partial(_post_kernel, tiles_per_seq=tps, seq_len=seq_len),
        out_shape=jax.ShapeDtypeStruct((n_tok, d), F32),
        grid=(n_tok // tm,),
        in_specs=[
            pl.BlockSpec((tm, d), lambda i: (i, 0)),
            pl.BlockSpec((halo, d), lambda i: (jnp.maximum(i * hb - 1, 0), 0)),
            pl.BlockSpec((halo, d), lambda i: (jnp.minimum((i + 1) * hb, n_hblocks - 1), 0)),
            pl.BlockSpec((tm, o_attn.shape[1]), lambda i: (i, 0)),
            pl.BlockSpec((None, N_MOD, d), lambda i: (i // tps, 0, 0)),
            _resident((1, d)),
            _resident(w_u.shape),
            _resident(w_gate.shape),
            _resident(w_pool.shape),
            _resident((1, w_u.shape[1])),
            _resident(w_out.shape),
            _resident((1, d)),
        ],
        out_specs=pl.BlockSpec((tm, d), lambda i: (i, 0)),
        scratch_shapes=[pltpu.VMEM((tm + 2 * halo, w_u.shape[1]), F32)],
        compiler_params=pltpu.CompilerParams(
            dimension_semantics=("arbitrary",),
            vmem_limit_bytes=_vmem_limit(est),
        ),
        name="mixer_post",
    )(x, x, x, o_attn, mod, n_pre, w_u, w_gate, w_pool, pool_scale, w_out, n_post)


def _rope_tables(seq_len):
    half = QK_ROPE_DIM // 2
    inv = 1.0 / (ROPE_THETA ** (jnp.arange(0, QK_ROPE_DIM, 2, dtype=F32) / QK_ROPE_DIM))
    ang = jnp.arange(seq_len, dtype=F32)[:, None] * inv[None, :]
    cos = jnp.cos(ang)
    sin = jnp.sin(ang)
    cos2 = jnp.concatenate([cos, cos, cos, cos], axis=-1)
    sin2 = jnp.concatenate([-sin, sin, -sin, sin], axis=-1)
    assert cos2.shape == (seq_len, LANES) and half * 4 == LANES
    return cos2, sin2


def _prepare_weights(w_in, w_qb, w_kvb, w_pool, w_out, w1_gu, w1_down, w2_gu, w2_down):
    n_layers = w_in.shape[0]
    o1 = Q_LORA_RANK
    o2 = o1 + KV_LORA_RANK
    o3 = o2 + QK_ROPE_DIM
    pool_w = w_pool.shape[1] * w_pool.shape[2]
    o4 = o3 + pool_w
    w_a = jnp.concatenate([w_in[:, :, :o3], w_in[:, :, o2:o3]], axis=-1).astype(BF16)
    w_u = w_in[:, :, o3:o4].astype(BF16)
    w_gate = w_in[:, :, o4:].astype(BF16)
    qb = w_qb.reshape(n_layers, Q_LORA_RANK, N_HEADS, QK_NOPE_DIM + QK_ROPE_DIM)
    w_qb2 = jnp.concatenate([qb[..., :QK_NOPE_DIM].reshape(n_layers, Q_LORA_RANK, -1),
                             qb[..., QK_NOPE_DIM:].reshape(n_layers, Q_LORA_RANK, -1)], axis=-1).astype(BF16)
    kvb = w_kvb.reshape(n_layers, KV_LORA_RANK, N_HEADS, QK_NOPE_DIM + V_HEAD_DIM)
    w_k = kvb[..., :QK_NOPE_DIM].reshape(n_layers, KV_LORA_RANK, -1).astype(BF16)
    w_vt = jnp.swapaxes(kvb[..., QK_NOPE_DIM:].reshape(n_layers, KV_LORA_RANK, -1), 1, 2).astype(BF16)
    return dict(w_a=w_a, w_u=w_u, w_gate=w_gate, w_qb=w_qb2, w_k=w_k, w_vt=w_vt,
                w_pool=w_pool.astype(BF16), w_out=w_out.astype(BF16),
                w1_gu=w1_gu.astype(BF16), w1_down=w1_down.astype(BF16),
                w2_gu=w2_gu.astype(BF16), w2_down=w2_down.astype(BF16))


def kernel(x_prompt, x_sample, c_prompt, c_sample, w_ada, b_ada, n1_pre, w1_gu, w1_down, n1_post, nm_pre, w_in, q_a_norm, w_qb, kv_a_norm, w_kvb, w_pool, pool_scale, w_out, nm_post, n2_pre, w2_gu, w2_down, n2_post):
    n_layers, d = n1_pre.shape
    seq_len = x_prompt.shape[1]
    assert x_sample.shape[1] == seq_len and seq_len % TOKEN_TILE == 0 and seq_len % KV_CHUNK == 0
    assert TOKEN_TILE % POOL_HALO == 0 and seq_len % Q_TILE == 0
    nb_p, nb_s = x_prompt.shape[0], x_sample.shape[0]
    n_seq = nb_p + nb_s
    tps = seq_len // TOKEN_TILE

    x = jnp.concatenate([x_prompt.reshape(-1, d), x_sample.reshape(-1, d)], axis=0)
    c = jnp.concatenate([c_prompt, c_sample], axis=0)
    bp = -(-n_seq // SUBLANES) * SUBLANES
    c = jnp.pad(c, ((0, bp - n_seq), (0, 0)))
    mod = _ada_modulation(c, w_ada, b_ada)
    mod = jnp.swapaxes(mod, 1, 2)

    w = _prepare_weights(w_in, w_qb, w_kvb, w_pool, w_out, w1_gu, w1_down, w2_gu, w2_down)
    cos2, sin2 = _rope_tables(seq_len)
    row = lambda a, l: a[l][None, :]

    for l in range(n_layers):
        x = _ffn(x, mod[l], row(n1_pre, l), w["w1_gu"][l], w["w1_down"][l], row(n1_post, l),
                 row0=0, tiles_per_seq=tps)
        q, k, v_t = _qkv(x, mod[l], row(nm_pre, l), w["w_a"][l], row(q_a_norm, l), w["w_qb"][l],
                         row(kv_a_norm, l), w["w_k"][l], w["w_vt"][l], cos2, sin2,
                         n_seq=n_seq, seq_len=seq_len)
        o_attn = _attention(q, k, v_t).reshape(n_seq * seq_len, -1)
        x = _post(x, o_attn, mod[l], row(nm_pre, l), w["w_u"][l], w["w_gate"][l], w["w_pool"][l],
                  row(pool_scale, l), w["w_out"][l], row(nm_post, l), seq_len=seq_len)
        x = _ffn(x, mod[l], row(n2_pre, l), w["w2_gu"][l], w["w2_down"][l], row(n2_post, l),
                 row0=6, tiles_per_seq=tps)

    x = x.reshape(n_seq, seq_len, d)
    return (x[:nb_p], x[nb_p:])
```

```python
import functools
import math

import jax
import jax.numpy as jnp
from jax import lax
from jax.experimental import pallas as pl
from jax.experimental.pallas import tpu as pltpu

N_HEADS = 8
QK_NOPE_DIM = 128
QK_ROPE_DIM = 64
V_HEAD_DIM = 128
Q_LORA_RANK = 256
KV_LORA_RANK = 128
POOL_WINDOWS = (2, 4, 8, 16)
ROPE_THETA = 10000.0
N_MOD = 9
EPS = 1e-6

QK_WIDTH = QK_NOPE_DIM + 2 * QK_ROPE_DIM
POOL_HALO = max(POOL_WINDOWS) // 2

V7X_VMEM_BYTES = 64 * 1024 * 1024
V7X_VMEM_REQUEST_CAP = 56 * 1024 * 1024
SUBLANES = 8
LANES = 128

TOKEN_TILE = 512
Q_TILE = 512
KV_CHUNK = 1024
FFN_CHUNKS = 2

F32 = jnp.float32
BF16 = jnp.bfloat16


def _vmem_limit(estimate_bytes):
    return int(min(max(estimate_bytes, 16 * 1024 * 1024), V7X_VMEM_REQUEST_CAP))


def _resident(shape):
    zeros = (0,) * len(shape)
    return pl.BlockSpec(shape, lambda *_: zeros, pipeline_mode=pl.Buffered(1))


def _rmsnorm(x, g):
    return x * lax.rsqrt(jnp.mean(x * x, axis=-1, keepdims=True) + EPS) * g


def _sigmoid(x):
    return 1.0 / (1.0 + jnp.exp(-x))


def _dot(a, b):
    return jnp.dot(a, b, preferred_element_type=F32)


def _ada_kernel(c_ref, w_ref, b_ref, o_ref):
    c = c_ref[...]
    a = c * _sigmoid(c)
    w = w_ref[...]
    a_hi = a.astype(BF16)
    a_lo = (a - a_hi.astype(F32)).astype(BF16)
    w_hi = w.astype(BF16)
    w_lo = (w - w_hi.astype(F32)).astype(BF16)
    o_ref[...] = _dot(a_hi, w_hi) + (_dot(a_lo, w_hi) + _dot(a_hi, w_lo)) + b_ref[...]


def _ada_modulation(c, w_ada, b_ada):
    n_layers, d, _ = w_ada.shape
    bp = c.shape[0]
    return pl.pallas_call(
        _ada_kernel,
        out_shape=jax.ShapeDtypeStruct((n_layers, N_MOD, bp, d), F32),
        grid=(n_layers, N_MOD),
        in_specs=[
            pl.BlockSpec((bp, d), lambda l, j: (0, 0)),
            pl.BlockSpec((None, d, d), lambda l, j: (l, 0, j)),
            pl.BlockSpec((None, 1, d), lambda l, j: (l, 0, j)),
        ],
        out_specs=pl.BlockSpec((None, None, bp, d), lambda l, j: (l, j, 0, 0)),
        compiler_params=pltpu.CompilerParams(
            dimension_semantics=("arbitrary", "arbitrary"),
            vmem_limit_bytes=_vmem_limit(4 * d * d * 4),
        ),
        name="ada_modulation",
    )(c, w_ada, b_ada.reshape(n_layers, 1, N_MOD * d))


def _ffn_kernel(x_ref, mod_ref, npre_ref, wgu_ref, wd_ref, npost_ref, o_ref, *, row0, d_ff):
    x = x_ref[...]
    shift = mod_ref[row0:row0 + 1, :]
    scale = mod_ref[row0 + 1:row0 + 2, :]
    gate = mod_ref[row0 + 2:row0 + 3, :]
    h = (_rmsnorm(x, npre_ref[...]) * (1.0 + scale) + shift).astype(BF16)
    fc = d_ff // FFN_CHUNKS
    acc = None
    for c in range(FFN_CHUNKS):
        g = _dot(h, wgu_ref[:, c * fc:(c + 1) * fc])
        u = _dot(h, wgu_ref[:, d_ff + c * fc:d_ff + (c + 1) * fc])
        a = ((g * _sigmoid(g)) * u).astype(BF16)
        part = _dot(a, wd_ref[c * fc:(c + 1) * fc, :])
        acc = part if acc is None else acc + part
    o_ref[...] = x + (0.5 * gate) * _rmsnorm(acc, npost_ref[...])


def _ffn(x, mod, n_pre, w_gu, w_down, n_post, *, row0, tiles_per_seq):
    n_tok, d = x.shape
    d_ff = w_down.shape[0]
    tm = TOKEN_TILE
    est = (w_gu.size + w_down.size) * 2 + 4 * tm * d * 4 + 3 * tm * (d_ff // FFN_CHUNKS) * 4 * 2
    return pl.pallas_call(
        functools.partial(_ffn_kernel, row0=row0, d_ff=d_ff),
        out_shape=jax.ShapeDtypeStruct((n_tok, d), F32),
        grid=(n_tok // tm,),
        in_specs=[
            pl.BlockSpec((tm, d), lambda i: (i, 0)),
            pl.BlockSpec((None, N_MOD, d), lambda i: (i // tiles_per_seq, 0, 0)),
            _resident((1, d)),
            _resident(w_gu.shape),
            _resident(w_down.shape),
            _resident((1, d)),
        ],
        out_specs=pl.BlockSpec((tm, d), lambda i: (i, 0)),
        compiler_params=pltpu.CompilerParams(
            dimension_semantics=("arbitrary",),
            vmem_limit_bytes=_vmem_limit(est),
        ),
        name="ffn_half_step",
    )(x, mod, n_pre, w_gu, w_down, n_post)


def _rope(x, cos, sin_signed, first_half):
    rot = jnp.where(first_half, pltpu.roll(x, LANES - QK_ROPE_DIM // 2, axis=1),
                    pltpu.roll(x, QK_ROPE_DIM // 2, axis=1))
    return x * cos + rot * sin_signed


def _qkv_kernel(x_ref, mod_ref, npre_ref, wa_ref, qn_ref, wqb_ref, kvn_ref, wk_ref, wvt_ref,
                cos_ref, sin_ref, q_ref, k_ref, vt_ref, *, q_scale):
    x = x_ref[...]
    shift = mod_ref[3:4, :]
    scale = mod_ref[4:5, :]
    h = (_rmsnorm(x, npre_ref[...]) * (1.0 + scale) + shift).astype(BF16)
    za = _dot(h, wa_ref[...])
    q_a = za[:, :Q_LORA_RANK]
    kv_a = za[:, Q_LORA_RANK:Q_LORA_RANK + KV_LORA_RANK]
    kr = za[:, Q_LORA_RANK + KV_LORA_RANK:]

    cos = cos_ref[...]
    sin_signed = sin_ref[...]
    lane = lax.broadcasted_iota(jnp.int32, (1, LANES), 1)
    first_half = (lane % QK_ROPE_DIM) < (QK_ROPE_DIM // 2)
    slot_a = lane < QK_ROPE_DIM

    qn = _rmsnorm(q_a, qn_ref[...]).astype(BF16)
    q = _dot(qn, wqb_ref[...]) * q_scale
    kvn = _rmsnorm(kv_a, kvn_ref[...]).astype(BF16)
    k_nope = _dot(kvn, wk_ref[...])
    v_t = lax.dot_general(wvt_ref[...], kvn, (((1,), (1,)), ((), ())),
                          preferred_element_type=F32)
    kr = _rope(kr, cos, sin_signed, first_half).astype(BF16)

    nope_all = N_HEADS * QK_NOPE_DIM
    for pair in range(N_HEADS // 2):
        qr = _rope(q[:, nope_all + pair * LANES:nope_all + (pair + 1) * LANES], cos, sin_signed, first_half)
        for sub in range(2):
            hd = 2 * pair + sub
            q_ref[hd, :, :QK_NOPE_DIM] = q[:, hd * QK_NOPE_DIM:(hd + 1) * QK_NOPE_DIM].astype(BF16)
            keep = slot_a if sub == 0 else jnp.logical_not(slot_a)
            q_ref[hd, :, QK_NOPE_DIM:] = jnp.where(keep, qr, 0.0).astype(BF16)
    for hd in range(N_HEADS):
        k_ref[hd, :, :QK_NOPE_DIM] = k_nope[:, hd * QK_NOPE_DIM:(hd + 1) * QK_NOPE_DIM].astype(BF16)
        k_ref[hd, :, QK_NOPE_DIM:] = kr
        vt_ref[hd] = v_t[hd * V_HEAD_DIM:(hd + 1) * V_HEAD_DIM, :].astype(BF16)


def _qkv(x, mod, n_pre, w_a, q_norm, w_qb, kv_norm, w_k, w_vt, cos2, sin2, *, n_seq, seq_len):
    n_tok, d = x.shape
    tm = TOKEN_TILE
    tps = seq_len // tm
    q_scale = (QK_NOPE_DIM + QK_ROPE_DIM) ** -0.5 * math.log2(math.e)
    weights = (w_a, w_qb, w_k, w_vt)
    est = sum(w.size for w in weights) * 2 + 2 * tm * d * 4 + 4 * 3 * N_HEADS * tm * QK_WIDTH * 2 + 8 * tm * 2048 * 4
    return pl.pallas_call(
        functools.partial(_qkv_kernel, q_scale=q_scale),
        out_shape=(
            jax.ShapeDtypeStruct((n_seq, N_HEADS, seq_len, QK_WIDTH), BF16),
            jax.ShapeDtypeStruct((n_seq, N_HEADS, seq_len, QK_WIDTH), BF16),
            jax.ShapeDtypeStruct((n_seq, N_HEADS, V_HEAD_DIM, seq_len), BF16),
        ),
        grid=(n_tok // tm,),
        in_specs=[
            pl.BlockSpec((tm, d), lambda i: (i, 0)),
            pl.BlockSpec((None, N_MOD, d), lambda i: (i // tps, 0, 0)),
            _resident((1, d)),
            _resident(w_a.shape),
            _resident((1, Q_LORA_RANK)),
            _resident(w_qb.shape),
            _resident((1, KV_LORA_RANK)),
            _resident(w_k.shape),
            _resident(w_vt.shape),
            pl.BlockSpec((tm, LANES), lambda i: (i % tps, 0)),
            pl.BlockSpec((tm, LANES), lambda i: (i % tps, 0)),
        ],
        out_specs=(
            pl.BlockSpec((None, N_HEADS, tm, QK_WIDTH), lambda i: (i // tps, 0, i % tps, 0)),
            pl.BlockSpec((None, N_HEADS, tm, QK_WIDTH), lambda i: (i // tps, 0, i % tps, 0)),
            pl.BlockSpec((None, N_HEADS, V_HEAD_DIM, tm), lambda i: (i // tps, 0, 0, i % tps)),
        ),
        compiler_params=pltpu.CompilerParams(
            dimension_semantics=("arbitrary",),
            vmem_limit_bytes=_vmem_limit(est),
        ),
        name="mixer_qkv",
    )(x, mod, n_pre, w_a, q_norm, w_qb, kv_norm, w_k, w_vt, cos2, sin2)


def _attn_kernel(q_ref, k_ref, vt_ref, o_ref, s_scr, m_scr, l_scr, acc_scr, *, seq_len):
    tq, tk = Q_TILE, KV_CHUNK
    nq = seq_len // tq
    n_steps = nq * (seq_len // tk)

    m_scr[...] = jnp.full(m_scr.shape, -jnp.inf, F32)
    l_scr[...] = jnp.zeros(l_scr.shape, F32)
    acc_scr[...] = jnp.zeros(acc_scr.shape, F32)

    def scores(t, slot):
        j = lax.div(t, nq)
        qi = lax.rem(t, nq)
        k = k_ref[pl.ds(pl.multiple_of(j * tk, tk), tk), :]
        q = q_ref[pl.ds(pl.multiple_of(qi * tq, tq), tq), :]
        s_t = lax.dot_general(k, q, (((1,), (1,)), ((), ())), preferred_element_type=F32)
        s_scr[slot] = s_t
        return jnp.max(s_t, axis=0, keepdims=True)

    def update(t, slot, chunk_max):
        j = lax.div(t, nq)
        qi = lax.rem(t, nq)
        cols = pl.ds(pl.multiple_of(qi * tq, tq), tq)
        m_old = m_scr[:, cols]
        m_new = jnp.maximum(m_old, chunk_max)
        alpha = jnp.exp2(m_old - m_new)
        p_t = jnp.exp2(s_scr[slot] - m_new)
        l_scr[:, cols] = alpha * l_scr[:, cols] + jnp.sum(p_t, axis=0, keepdims=True)
        v_t = vt_ref[:, pl.ds(pl.multiple_of(j * tk, tk), tk)]
        acc_scr[:, cols] = alpha * acc_scr[:, cols] + _dot(v_t, p_t.astype(BF16))
        m_scr[:, cols] = m_new

    def pair(i, max0):
        t = 2 * i
        max1 = scores(t + 1, 1)
        update(t, 0, max0)
        max2 = scores(t + 2, 0)
        update(t + 1, 1, max1)
        return max2

    max0 = lax.fori_loop(0, n_steps // 2 - 1, pair, scores(0, 0))
    max1 = scores(n_steps - 1, 1)
    update(n_steps - 2, 0, max0)
    update(n_steps - 1, 1, max1)

    def finish(qi, carry):
        rows = pl.ds(pl.multiple_of(qi * tq, tq), tq)
        o_ref[rows, :] = (acc_scr[:, rows] / l_scr[:, rows]).T
        return carry

    lax.fori_loop(0, nq, finish, 0)


def _attention(q, k, v_t):
    n_seq, n_heads, seq_len, _ = q.shape
    assert (seq_len // Q_TILE) * (seq_len // KV_CHUNK) % 2 == 0
    in_bytes = 2 * seq_len * QK_WIDTH * 2 + seq_len * V_HEAD_DIM * 2
    out_bytes = seq_len * V_HEAD_DIM * 4
    scratch_bytes = 2 * KV_CHUNK * Q_TILE * 4 + V_HEAD_DIM * seq_len * 4 + 2 * SUBLANES * seq_len * 4
    est = 2 * (in_bytes + out_bytes) + scratch_bytes + 4 * KV_CHUNK * Q_TILE * 4
    return pl.pallas_call(
        functools.partial(_attn_kernel, seq_len=seq_len),
        out_shape=jax.ShapeDtypeStruct((n_seq, seq_len, n_heads * V_HEAD_DIM), F32),
        grid=(n_seq, n_heads),
        in_specs=[
            pl.BlockSpec((None, None, seq_len, QK_WIDTH), lambda b, h: (b, h, 0, 0)),
            pl.BlockSpec((None, None, seq_len, QK_WIDTH), lambda b, h: (b, h, 0, 0)),
            pl.BlockSpec((None, None, V_HEAD_DIM, seq_len), lambda b, h: (b, h, 0, 0)),
        ],
        out_specs=pl.BlockSpec((None, seq_len, V_HEAD_DIM), lambda b, h: (b, 0, h)),
        scratch_shapes=[
            pltpu.VMEM((2, KV_CHUNK, Q_TILE), F32),
            pltpu.VMEM((1, seq_len), F32),
            pltpu.VMEM((1, seq_len), F32),
            pltpu.VMEM((V_HEAD_DIM, seq_len), F32),
        ],
        compiler_params=pltpu.CompilerParams(
            dimension_semantics=("arbitrary", "arbitrary"),
            vmem_limit_bytes=_vmem_limit(est),
        ),
        name="latent_attention",
    )(q, k, v_t)


def _post_kernel(x_ref, xprev_ref, xnext_ref, oattn_ref, mod_ref, npre_ref, wu_ref, wg_ref, wpool_ref,
                 pscale_ref, wout_ref, npost_ref, o_ref, e_scr, *, tiles_per_seq, seq_len):
    tm, d = x_ref.shape
    halo = POOL_HALO
    j = pl.program_id(0) % tiles_per_seq
    shift = mod_ref[3:4, :]
    scale = mod_ref[4:5, :]
    gate = mod_ref[5:6, :]
    npre = npre_ref[...]

    def mod_norm(v):
        return _rmsnorm(v, npre) * (1.0 + scale) + shift

    x = x_ref[...]
    h_ext = jnp.concatenate([mod_norm(xprev_ref[...]), mod_norm(x), mod_norm(xnext_ref[...])], axis=0).astype(BF16)
    u_ext = _dot(h_ext, wu_ref[...])
    row = lax.broadcasted_iota(jnp.int32, (tm + 2 * halo, 1), 0)
    outside = jnp.logical_or(jnp.logical_and(j == 0, row < halo),
                             jnp.logical_and(j == tiles_per_seq - 1, row >= tm + halo))
    e_scr[...] = jnp.where(outside, 0.0, u_ext)

    pool_w = wu_ref.shape[1]
    group = pool_w // len(POOL_WINDOWS)
    t = j * tm + lax.broadcasted_iota(jnp.int32, (tm, 1), 0)
    mixed = []
    for g, window in enumerate(POOL_WINDOWS):
        hw = window // 2
        cols = slice(g * group, (g + 1) * group)
        win = e_scr[pl.ds(halo - hw, tm), cols]
        for off in range(halo - hw + 1, halo + hw):
            win = win + e_scr[pl.ds(off, tm), cols]
        count = (jnp.minimum(t + hw, seq_len) - jnp.maximum(t - hw, 0)).astype(F32)
        pooled = (win / count - e_scr[pl.ds(halo, tm), cols]).astype(BF16)
        mixed.append(_dot(pooled, wpool_ref[g]))
    o_pool = jnp.concatenate(mixed, axis=1) * pscale_ref[...]

    h = h_ext[halo:halo + tm, :]
    gates = _sigmoid(_dot(h, wg_ref[...]))
    attn_w = oattn_ref.shape[1]
    merged = (gates[:, :attn_w] * oattn_ref[...] + gates[:, attn_w:] * o_pool).astype(BF16)
    m = _dot(merged, wout_ref[...])
    o_ref[...] = x + gate * _rmsnorm(m, npost_ref[...])


def _post(x, o_attn, mod, n_pre, w_u, w_gate, w_pool, pool_scale, w_out, n_post, *, seq_len):
    n_tok, d = x.shape
    tm = TOKEN_TILE
    tps = seq_len // tm
    halo = POOL_HALO
    hb = tm // halo
    n_hblocks = n_tok // halo
    weights = (w_u, w_gate, w_pool, w_out)
    est = sum(w.size for w in weights) * 2 + 6 * tm * d * 4 + 10 * tm * 2048 * 4
    return pl.pallas_call(
        functools.partial(_post_kernel, tiles_per_seq=tps, seq_len=seq_len),
        out_shape=jax.ShapeDtypeStruct((n_tok, d), F32),
        grid=(n_tok // tm,),
        in_specs=[
            pl.BlockSpec((tm, d), lambda i: (i, 0)),
            pl.BlockSpec((halo, d), lambda i: (jnp.maximum(i * hb - 1, 0), 0)),
            pl.BlockSpec((halo, d), lambda i: (jnp.minimum((i + 1) * hb, n_hblocks - 1), 0)),
            pl.BlockSpec((tm, o_attn.shape[1]), lambda i: (i, 0)),
            pl.BlockSpec((None, N_MOD, d), lambda i: (i // tps, 0, 0)),
            _resident((1, d)),
            _resident(w_u.shape),
            _resident(w_gate.shape),
            _resident(w_pool.shape),
            _resident((1, w_u.shape[1])),
            _resident(w_out.shape),
            _resident((1, d)),
        ],
        out_specs=pl.BlockSpec((tm, d), lambda i: (i, 0)),
        scratch_shapes=[pltpu.VMEM((tm + 2 * halo, w_u.shape[1]), F32)],
        compiler_params=pltpu.CompilerParams(
            dimension_semantics=("arbitrary",),
            vmem_limit_bytes=_vmem_limit(est),
        ),
        name="mixer_post",
    )(x, x, x, o_attn, mod, n_pre, w_u, w_gate, w_pool, pool_scale, w_out, n_post)


def _rope_tables(seq_len):
    half = QK_ROPE_DIM // 2
    inv = 1.0 / (ROPE_THETA ** (jnp.arange(0, QK_ROPE_DIM, 2, dtype=F32) / QK_ROPE_DIM))
    ang = jnp.arange(seq_len, dtype=F32)[:, None] * inv[None, :]
    cos = jnp.cos(ang)
    sin = jnp.sin(ang)
    cos2 = jnp.concatenate([cos, cos, cos, cos], axis=-1)
    sin2 = jnp.concatenate([-sin, sin, -sin, sin], axis=-1)
    assert cos2.shape == (seq_len, LANES) and half * 4 == LANES
    return cos2, sin2


def _prepare_weights(w_in, w_qb, w_kvb, w_pool, w_out, w1_gu, w1_down, w2_gu, w2_down):
    n_layers = w_in.shape[0]
    o1 = Q_LORA_RANK
    o2 = o1 + KV_LORA_RANK
    o3 = o2 + QK_ROPE_DIM
    pool_w = w_pool.shape[1] * w_pool.shape[2]
    o4 = o3 + pool_w
    w_a = jnp.concatenate([w_in[:, :, :o3], w_in[:, :, o2:o3]], axis=-1).astype(BF16)
    w_u = w_in[:, :, o3:o4].astype(BF16)
    w_gate = w_in[:, :, o4:].astype(BF16)
    qb = w_qb.reshape(n_layers, Q_LORA_RANK, N_HEADS, QK_NOPE_DIM + QK_ROPE_DIM)
    w_qb2 = jnp.concatenate([qb[..., :QK_NOPE_DIM].reshape(n_layers, Q_LORA_RANK, -1),
                             qb[..., QK_NOPE_DIM:].reshape(n_layers, Q_LORA_RANK, -1)], axis=-1).astype(BF16)
    kvb = w_kvb.reshape(n_layers, KV_LORA_RANK, N_HEADS, QK_NOPE_DIM + V_HEAD_DIM)
    w_k = kvb[..., :QK_NOPE_DIM].reshape(n_layers, KV_LORA_RANK, -1).astype(BF16)
    w_vt = jnp.swapaxes(kvb[..., QK_NOPE_DIM:].reshape(n_layers, KV_LORA_RANK, -1), 1, 2).astype(BF16)
    return dict(w_a=w_a, w_u=w_u, w_gate=w_gate, w_qb=w_qb2, w_k=w_k, w_vt=w_vt,
                w_pool=w_pool.astype(BF16), w_out=w_out.astype(BF16),
                w1_gu=w1_gu.astype(BF16), w1_down=w1_down.astype(BF16),
                w2_gu=w2_gu.astype(BF16), w2_down=w2_down.astype(BF16))


def kernel(x_prompt, x_sample, c_prompt, c_sample, w_ada, b_ada, n1_pre, w1_gu, w1_down, n1_post, nm_pre, w_in, q_a_norm, w_qb, kv_a_norm, w_kvb, w_pool, pool_scale, w_out, nm_post, n2_pre, w2_gu, w2_down, n2_post):
    n_layers, d = n1_pre.shape
    seq_len = x_prompt.shape[1]
    assert x_sample.shape[1] == seq_len and seq_len % TOKEN_TILE == 0 and seq_len % KV_CHUNK == 0
    assert TOKEN_TILE % POOL_HALO == 0 and seq_len % Q_TILE == 0
    nb_p, nb_s = x_prompt.shape[0], x_sample.shape[0]
    n_seq = nb_p + nb_s
    tps = seq_len // TOKEN_TILE

    x = jnp.concatenate([x_prompt.reshape(-1, d), x_sample.reshape(-1, d)], axis=0)
    c = jnp.concatenate([c_prompt, c_sample], axis=0)
    bp = -(-n_seq // SUBLANES) * SUBLANES
    c = jnp.pad(c, ((0, bp - n_seq), (0, 0)))
    mod = _ada_modulation(c, w_ada, b_ada)
    mod = jnp.swapaxes(mod, 1, 2)

    w = _prepare_weights(w_in, w_qb, w_kvb, w_pool, w_out, w1_gu, w1_down, w2_gu, w2_down)
    cos2, sin2 = _rope_tables(seq_len)
    row = lambda a, l: a[l][None, :]

    for l in range(n_layers):
        x = _ffn(x, mod[l], row(n1_pre, l), w["w1_gu"][l], w["w1_down"][l], row(n1_post, l),
                 row0=0, tiles_per_seq=tps)
        q, k, v_t = _qkv(x, mod[l], row(nm_pre, l), w["w_a"][l], row(q_a_norm, l), w["w_qb"][l],
                         row(kv_a_norm, l), w["w_k"][l], w["w_vt"][l], cos2, sin2,
                         n_seq=n_seq, seq_len=seq_len)
        o_attn = _attention(q, k, v_t).reshape(n_seq * seq_len, -1)
        x = _post(x, o_attn, mod[l], row(nm_pre, l), w["w_u"][l], w["w_gate"][l], w["w_pool"][l],
                  row(pool_scale, l), w["w_out"][l], row(nm_post, l), seq_len=seq_len)
        x = _ffn(x, mod[l], row(n2_pre, l), w["w2_gu"][l], w["w2_down"][l], row(n2_post, l),
                 row0=6, tiles_per_seq=tps)

    x = x.reshape(n_seq, seq_len, d)
    return (x[:nb_p], x[nb_p:])
```

```python
import functools
import math

import jax
import jax.numpy as jnp
from jax import lax
from jax.experimental import pallas as pl
from jax.experimental.pallas import tpu as pltpu

N_HEADS = 8
QK_NOPE_DIM = 128
QK_ROPE_DIM = 64
V_HEAD_DIM = 128
Q_LORA_RANK = 256
KV_LORA_RANK = 128
POOL_WINDOWS = (2, 4, 8, 16)
ROPE_THETA = 10000.0
N_MOD = 9
EPS = 1e-6

QK_WIDTH = QK_NOPE_DIM + 2 * QK_ROPE_DIM
POOL_HALO = max(POOL_WINDOWS) // 2

V7X_VMEM_BYTES = 64 * 1024 * 1024
V7X_VMEM_REQUEST_CAP = 56 * 1024 * 1024
SUBLANES = 8
LANES = 128

TOKEN_TILE = 512
Q_TILE = 512
KV_CHUNK = 1024
ATTN_UNROLL = 4
ATTN_BATCHES = 1
FFN_TILE = 1024
FFN_SUBTILES = 4
FFN_CHUNKS = 1

F32 = jnp.float32
BF16 = jnp.bfloat16


def _vmem_limit(estimate_bytes):
    return int(min(max(estimate_bytes, 16 * 1024 * 1024), V7X_VMEM_REQUEST_CAP))


def _resident(shape):
    zeros = (0,) * len(shape)
    return pl.BlockSpec(shape, lambda *_: zeros, pipeline_mode=pl.Buffered(1))


def _rmsnorm(x, g):
    return x * lax.rsqrt(jnp.mean(x * x, axis=-1, keepdims=True) + EPS) * g


def _sigmoid(x):
    return 1.0 / (1.0 + jnp.exp(-x))


def _dot(a, b):
    return jnp.dot(a, b, preferred_element_type=F32)


def _ada_kernel(c_ref, w_ref, b_ref, o_ref):
    c = c_ref[...]
    a = c * _sigmoid(c)
    w = w_ref[...]
    a_hi = a.astype(BF16)
    a_lo = (a - a_hi.astype(F32)).astype(BF16)
    w_hi = w.astype(BF16)
    w_lo = (w - w_hi.astype(F32)).astype(BF16)
    o_ref[...] = _dot(a_hi, w_hi) + (_dot(a_lo, w_hi) + _dot(a_hi, w_lo)) + b_ref[...]


def _ada_modulation(c, w_ada, b_ada):
    n_layers, d, _ = w_ada.shape
    bp = c.shape[0]
    return pl.pallas_call(
        _ada_kernel,
        out_shape=jax.ShapeDtypeStruct((n_layers, N_MOD, bp, d), F32),
        grid=(n_layers, N_MOD),
        in_specs=[
            pl.BlockSpec((bp, d), lambda l, j: (0, 0)),
            pl.BlockSpec((None, d, d), lambda l, j: (l, 0, j)),
            pl.BlockSpec((None, 1, d), lambda l, j: (l, 0, j)),
        ],
        out_specs=pl.BlockSpec((None, None, bp, d), lambda l, j: (l, j, 0, 0)),
        compiler_params=pltpu.CompilerParams(
            dimension_semantics=("arbitrary", "arbitrary"),
            vmem_limit_bytes=_vmem_limit(4 * d * d * 4),
        ),
        name="ada_modulation",
    )(c, w_ada, b_ada.reshape(n_layers, 1, N_MOD * d))


def _ffn_kernel(x_ref, mod_ref, npre_ref, wgu_ref, wd_ref, npost_ref, o_ref, *, row0, d_ff):
    shift = mod_ref[row0:row0 + 1, :]
    scale = mod_ref[row0 + 1:row0 + 2, :]
    gate = mod_ref[row0 + 2:row0 + 3, :]
    fc = d_ff // FFN_CHUNKS
    rows = x_ref.shape[0] // FFN_SUBTILES
    for r in range(FFN_SUBTILES):
        x = x_ref[r * rows:(r + 1) * rows, :]
        h = (_rmsnorm(x, npre_ref[...]) * (1.0 + scale) + shift).astype(BF16)
        acc = None
        for c in range(FFN_CHUNKS):
            g = _dot(h, wgu_ref[:, c * fc:(c + 1) * fc])
            u = _dot(h, wgu_ref[:, d_ff + c * fc:d_ff + (c + 1) * fc])
            a = ((g * _sigmoid(g)) * u).astype(BF16)
            part = _dot(a, wd_ref[c * fc:(c + 1) * fc, :])
            acc = part if acc is None else acc + part
        o_ref[r * rows:(r + 1) * rows, :] = x + (0.5 * gate) * _rmsnorm(acc, npost_ref[...])


def _ffn(x, mod, n_pre, w_gu, w_down, n_post, *, row0, seq_len):
    n_tok, d = x.shape
    d_ff = w_down.shape[0]
    tm = FFN_TILE
    tiles_per_seq = seq_len // tm
    sub_rows = tm // FFN_SUBTILES
    est = (w_gu.size + w_down.size) * 2 + 4 * tm * d * 4 + 2 * 3 * sub_rows * (d_ff // FFN_CHUNKS) * 4 * 2
    return pl.pallas_call(
        functools.partial(_ffn_kernel, row0=row0, d_ff=d_ff),
        out_shape=jax.ShapeDtypeStruct((n_tok, d), F32),
        grid=(n_tok // tm,),
        in_specs=[
            pl.BlockSpec((tm, d), lambda i: (i, 0)),
            pl.BlockSpec((None, N_MOD, d), lambda i: (i // tiles_per_seq, 0, 0)),
            _resident((1, d)),
            _resident(w_gu.shape),
            _resident(w_down.shape),
            _resident((1, d)),
        ],
        out_specs=pl.BlockSpec((tm, d), lambda i: (i, 0)),
        compiler_params=pltpu.CompilerParams(
            dimension_semantics=("arbitrary",),
            vmem_limit_bytes=_vmem_limit(est),
        ),
        name="ffn_half_step",
    )(x, mod, n_pre, w_gu, w_down, n_post)


def _rope(x, cos, sin_signed, first_half):
    rot = jnp.where(first_half, pltpu.roll(x, LANES - QK_ROPE_DIM // 2, axis=1),
                    pltpu.roll(x, QK_ROPE_DIM // 2, axis=1))
    return x * cos + rot * sin_signed


def _qkv_kernel(x_ref, mod_ref, npre_ref, wa_ref, qn_ref, wqb_ref, kvn_ref, wk_ref, wvt_ref,
                cos_ref, sin_ref, q_ref, k_ref, vt_ref, *, q_scale):
    x = x_ref[...]
    shift = mod_ref[3:4, :]
    scale = mod_ref[4:5, :]
    h = (_rmsnorm(x, npre_ref[...]) * (1.0 + scale) + shift).astype(BF16)
    za = _dot(h, wa_ref[...])
    q_a = za[:, :Q_LORA_RANK]
    kv_a = za[:, Q_LORA_RANK:Q_LORA_RANK + KV_LORA_RANK]
    kr = za[:, Q_LORA_RANK + KV_LORA_RANK:]

    cos = cos_ref[...]
    sin_signed = sin_ref[...]
    lane = lax.broadcasted_iota(jnp.int32, (1, LANES), 1)
    first_half = (lane % QK_ROPE_DIM) < (QK_ROPE_DIM // 2)
    slot_a = lane < QK_ROPE_DIM

    qn = _rmsnorm(q_a, qn_ref[...]).astype(BF16)
    q = _dot(qn, wqb_ref[...]) * q_scale
    kvn = _rmsnorm(kv_a, kvn_ref[...]).astype(BF16)
    k_nope = _dot(kvn, wk_ref[...])
    v_t = lax.dot_general(wvt_ref[...], kvn, (((1,), (1,)), ((), ())),
                          preferred_element_type=F32)
    kr = _rope(kr, cos, sin_signed, first_half).astype(BF16)

    nope_all = N_HEADS * QK_NOPE_DIM
    for pair in range(N_HEADS // 2):
        qr = _rope(q[:, nope_all + pair * LANES:nope_all + (pair + 1) * LANES], cos, sin_signed, first_half)
        for sub in range(2):
            hd = 2 * pair + sub
            q_ref[hd, :, :QK_NOPE_DIM] = q[:, hd * QK_NOPE_DIM:(hd + 1) * QK_NOPE_DIM].astype(BF16)
            keep = slot_a if sub == 0 else jnp.logical_not(slot_a)
            q_ref[hd, :, QK_NOPE_DIM:] = jnp.where(keep, qr, 0.0).astype(BF16)
    for hd in range(N_HEADS):
        k_ref[hd, :, :QK_NOPE_DIM] = k_nope[:, hd * QK_NOPE_DIM:(hd + 1) * QK_NOPE_DIM].astype(BF16)
        k_ref[hd, :, QK_NOPE_DIM:] = kr
        vt_ref[hd] = v_t[hd * V_HEAD_DIM:(hd + 1) * V_HEAD_DIM, :].astype(BF16)


def _qkv(x, mod, n_pre, w_a, q_norm, w_qb, kv_norm, w_k, w_vt, cos2, sin2, *, n_seq, seq_len):
    n_tok, d = x.shape
    tm = TOKEN_TILE
    tps = seq_len // tm
    q_scale = (QK_NOPE_DIM + QK_ROPE_DIM) ** -0.5 * math.log2(math.e)
    weights = (w_a, w_qb, w_k, w_vt)
    est = sum(w.size for w in weights) * 2 + 2 * tm * d * 4 + 4 * 3 * N_HEADS * tm * QK_WIDTH * 2 + 8 * tm * 2048 * 4
    return pl.pallas_call(
        functools.partial(_qkv_kernel, q_scale=q_scale),
        out_shape=(
            jax.ShapeDtypeStruct((n_seq, N_HEADS, seq_len, QK_WIDTH), BF16),
            jax.ShapeDtypeStruct((n_seq, N_HEADS, seq_len, QK_WIDTH), BF16),
            jax.ShapeDtypeStruct((n_seq, N_HEADS, V_HEAD_DIM, seq_len), BF16),
        ),
        grid=(n_tok // tm,),
        in_specs=[
            pl.BlockSpec((tm, d), lambda i: (i, 0)),
            pl.BlockSpec((None, N_MOD, d), lambda i: (i // tps, 0, 0)),
            _resident((1, d)),
            _resident(w_a.shape),
            _resident((1, Q_LORA_RANK)),
            _resident(w_qb.shape),
            _resident((1, KV_LORA_RANK)),
            _resident(w_k.shape),
            _resident(w_vt.shape),
            pl.BlockSpec((tm, LANES), lambda i: (i % tps, 0)),
            pl.BlockSpec((tm, LANES), lambda i: (i % tps, 0)),
        ],
        out_specs=(
            pl.BlockSpec((None, N_HEADS, tm, QK_WIDTH), lambda i: (i // tps, 0, i % tps, 0)),
            pl.BlockSpec((None, N_HEADS, tm, QK_WIDTH), lambda i: (i // tps, 0, i % tps, 0)),
            pl.BlockSpec((None, N_HEADS, V_HEAD_DIM, tm), lambda i: (i // tps, 0, 0, i % tps)),
        ),
        compiler_params=pltpu.CompilerParams(
            dimension_semantics=("arbitrary",),
            vmem_limit_bytes=_vmem_limit(est),
        ),
        name="mixer_qkv",
    )(x, mod, n_pre, w_a, q_norm, w_qb, kv_norm, w_k, w_vt, cos2, sin2)


def _attn_kernel(q_ref, k_ref, vt_ref, o_ref, s_scr, m_scr, l_scr, acc_scr, *, seq_len):
    tq, tk = Q_TILE, KV_CHUNK
    nq = seq_len // tq
    n_steps = nq * (seq_len // tk)

    m_scr[...] = jnp.full(m_scr.shape, -jnp.inf, F32)
    l_scr[...] = jnp.zeros(l_scr.shape, F32)
    acc_scr[...] = jnp.zeros(acc_scr.shape, F32)

    tb = tk // ATTN_BATCHES

    def scores_batch(t, slot, b):
        j = lax.div(t, nq)
        qi = lax.rem(t, nq)
        k = k_ref[pl.ds(pl.multiple_of(j * tk + b * tb, tb), tb), :]
        q = q_ref[pl.ds(pl.multiple_of(qi * tq, tq), tq), :]
        s_t = lax.dot_general(k, q, (((1,), (1,)), ((), ())), preferred_element_type=F32)
        s_scr[slot, b * tb:(b + 1) * tb, :] = s_t
        return jnp.max(s_t, axis=0, keepdims=True)

    def step(t, slot, chunk_max, with_next):
        j = lax.div(t, nq)
        qi = lax.rem(t, nq)
        cols = pl.ds(pl.multiple_of(qi * tq, tq), tq)
        m_old = m_scr[:, cols]
        m_new = jnp.maximum(m_old, chunk_max)
        alpha = jnp.exp2(m_old - m_new)
        next_max = None
        l_part = None
        pv = None
        for b in range(ATTN_BATCHES):
            if with_next:
                bm = scores_batch(t + 1, 1 - slot, b)
                next_max = bm if next_max is None else jnp.maximum(next_max, bm)
            p_t = jnp.exp2(s_scr[slot, b * tb:(b + 1) * tb, :] - m_new)
            ls = jnp.sum(p_t, axis=0, keepdims=True)
            l_part = ls if l_part is None else l_part + ls
            v_t = vt_ref[:, pl.ds(pl.multiple_of(j * tk + b * tb, tb), tb)]
            d = _dot(v_t, p_t.astype(BF16))
            pv = d if pv is None else pv + d
        l_scr[:, cols] = alpha * l_scr[:, cols] + l_part
        acc_scr[:, cols] = alpha * acc_scr[:, cols] + pv
        m_scr[:, cols] = m_new
        return next_max

    def first_scores():
        cm = None
        for b in range(ATTN_BATCHES):
            bm = scores_batch(0, 0, b)
            cm = bm if cm is None else jnp.maximum(cm, bm)
        return cm

    def group(t0, chunk_max, last):
        for u in range(ATTN_UNROLL):
            chunk_max = step(t0 + u, u % 2, chunk_max, not (last and u == ATTN_UNROLL - 1))
        return chunk_max

    chunk_max = lax.fori_loop(0, n_steps // ATTN_UNROLL - 1,
                              lambda i, cm: group(i * ATTN_UNROLL, cm, False), first_scores())
    group(n_steps - ATTN_UNROLL, chunk_max, True)

    def finish(qi, carry):
        rows = pl.ds(pl.multiple_of(qi * tq, tq), tq)
        o_ref[rows, :] = (acc_scr[:, rows] / l_scr[:, rows]).T
        return carry

    lax.fori_loop(0, nq, finish, 0)


def _attention(q, k, v_t):
    n_seq, n_heads, seq_len, _ = q.shape
    assert ATTN_UNROLL % 2 == 0 and (seq_len // Q_TILE) * (seq_len // KV_CHUNK) % ATTN_UNROLL == 0
    in_bytes = 2 * seq_len * QK_WIDTH * 2 + seq_len * V_HEAD_DIM * 2
    out_bytes = seq_len * V_HEAD_DIM * 4
    scratch_bytes = 2 * KV_CHUNK * Q_TILE * 4 + V_HEAD_DIM * seq_len * 4 + 2 * SUBLANES * seq_len * 4
    est = 2 * (in_bytes + out_bytes) + scratch_bytes + 4 * KV_CHUNK * Q_TILE * 4
    return pl.pallas_call(
        functools.partial(_attn_kernel, seq_len=seq_len),
        out_shape=jax.ShapeDtypeStruct((n_seq, seq_len, n_heads * V_HEAD_DIM), F32),
        grid=(n_seq, n_heads),
        in_specs=[
            pl.BlockSpec((None, None, seq_len, QK_WIDTH), lambda b, h: (b, h, 0, 0)),
            pl.BlockSpec((None, None, seq_len, QK_WIDTH), lambda b, h: (b, h, 0, 0)),
            pl.BlockSpec((None, None, V_HEAD_DIM, seq_len), lambda b, h: (b, h, 0, 0)),
        ],
        out_specs=pl.BlockSpec((None, seq_len, V_HEAD_DIM), lambda b, h: (b, 0, h)),
        scratch_shapes=[
            pltpu.VMEM((2, KV_CHUNK, Q_TILE), F32),
            pltpu.VMEM((1, seq_len), F32),
            pltpu.VMEM((1, seq_len), F32),
            pltpu.VMEM((V_HEAD_DIM, seq_len), F32),
        ],
        compiler_params=pltpu.CompilerParams(
            dimension_semantics=("arbitrary", "arbitrary"),
            vmem_limit_bytes=_vmem_limit(est),
        ),
        name="latent_attention",
    )(q, k, v_t)


def _post_kernel(x_ref, xprev_ref, xnext_ref, oattn_ref, mod_ref, npre_ref, wu_ref, wg_ref, wpool_ref,
                 pscale_ref, wout_ref, npost_ref, o_ref, *, tiles_per_seq, seq_len):
    tm, d = x_ref.shape
    halo = POOL_HALO
    j = pl.program_id(0) % tiles_per_seq
    shift = mod_ref[3:4, :]
    scale = mod_ref[4:5, :]
    gate = mod_ref[5:6, :]
    npre = npre_ref[...]

    def mod_norm(v):
        return _rmsnorm(v, npre) * (1.0 + scale) + shift

    x = x_ref[...]
    h_ext = jnp.concatenate([mod_norm(xprev_ref[...]), mod_norm(x), mod_norm(xnext_ref[...])], axis=0).astype(BF16)
    u_ext = _dot(h_ext, wu_ref[...])
    row = lax.broadcasted_iota(jnp.int32, (tm + 2 * halo, 1), 0)
    outside = jnp.logical_or(jnp.logical_and(j == 0, row < halo),
                             jnp.logical_and(j == tiles_per_seq - 1, row >= tm + halo))
    u_ext = jnp.where(outside, 0.0, u_ext)

    n_ext = tm + 2 * halo
    pool_w = wu_ref.shape[1]
    group = pool_w // len(POOL_WINDOWS)
    t = j * tm + lax.broadcasted_iota(jnp.int32, (tm, 1), 0)
    mixed = []
    for g, window in enumerate(POOL_WINDOWS):
        hw = window // 2
        e = u_ext[:, g * group:(g + 1) * group]
        win = e + pltpu.roll(e, 1, axis=0)
        step = 1
        while step < hw:
            win = pltpu.roll(win, step, axis=0) + pltpu.roll(win, n_ext - step, axis=0)
            step *= 2
        count = (jnp.minimum(t + hw, seq_len) - jnp.maximum(t - hw, 0)).astype(F32)
        pooled = (win[halo:halo + tm] / count - e[halo:halo + tm]).astype(BF16)
        mixed.append(_dot(pooled, wpool_ref[g]))
    o_pool = jnp.concatenate(mixed, axis=1) * pscale_ref[...]

    h = h_ext[halo:halo + tm, :]
    gates = _sigmoid(_dot(h, wg_ref[...]))
    attn_w = oattn_ref.shape[1]
    merged = (gates[:, :attn_w] * oattn_ref[...] + gates[:, attn_w:] * o_pool).astype(BF16)
    m = _dot(merged, wout_ref[...])
    o_ref[...] = x + gate * _rmsnorm(m, npost_ref[...])


def _post(x, o_attn, mod, n_pre, w_u, w_gate, w_pool, pool_scale, w_out, n_post, *, seq_len):
    n_tok, d = x.shape
    tm = TOKEN_TILE
    tps = seq_len // tm
    halo = POOL_HALO
    hb = tm // halo
    n_hblocks = n_tok // halo
    weights = (w_u, w_gate, w_pool, w_out)
    est = sum(w.size for w in weights) * 2 + 6 * tm * d * 4 + 10 * tm * 2048 * 4
    return pl.pallas_call(
        functools.partial(_post_kernel, tiles_per_seq=tps, seq_len=seq_len),
        out_shape=jax.ShapeDtypeStruct((n_tok, d), F32),
        grid=(n_tok // tm,),
        in_specs=[
            pl.BlockSpec((tm, d), lambda i: (i, 0)),
            pl.BlockSpec((halo, d), lambda i: (jnp.maximum(i * hb - 1, 0), 0)),
            pl.BlockSpec((halo, d), lambda i: (jnp.minimum((i + 1) * hb, n_hblocks - 1), 0)),
            pl.BlockSpec((tm, o_attn.shape[1]), lambda i: (i, 0)),
            pl.BlockSpec((None, N_MOD, d), lambda i: (i // tps, 0, 0)),
            _resident((1, d)),
            _resident(w_u.shape),
            _resident(w_gate.shape),
            _resident(w_pool.shape),
            _resident((1, w_u.shape[1])),
            _resident(w_out.shape),
            _resident((1, d)),
        ],
        out_specs=pl.BlockSpec((tm, d), lambda i: (i, 0)),
        compiler_params=pltpu.CompilerParams(
            dimension_semantics=("arbitrary",),
            vmem_limit_bytes=_vmem_limit(est),
        ),
        name="mixer_post",
    )(x, x, x, o_attn, mod, n_pre, w_u, w_gate, w_pool, pool_scale, w_out, n_post)


def _rope_tables(seq_len):
    half = QK_ROPE_DIM // 2
    inv = 1.0 / (ROPE_THETA ** (jnp.arange(0, QK_ROPE_DIM, 2, dtype=F32) / QK_ROPE_DIM))
    ang = jnp.arange(seq_len, dtype=F32)[:, None] * inv[None, :]
    cos = jnp.cos(ang)
    sin = jnp.sin(ang)
    cos2 = jnp.concatenate([cos, cos, cos, cos], axis=-1)
    sin2 = jnp.concatenate([-sin, sin, -sin, sin], axis=-1)
    assert cos2.shape == (seq_len, LANES) and half * 4 == LANES
    return cos2, sin2


def _prepare_weights(w_in, w_qb, w_kvb, w_pool, w_out, w1_gu, w1_down, w2_gu, w2_down):
    n_layers = w_in.shape[0]
    o1 = Q_LORA_RANK
    o2 = o1 + KV_LORA_RANK
    o3 = o2 + QK_ROPE_DIM
    pool_w = w_pool.shape[1] * w_pool.shape[2]
    o4 = o3 + pool_w
    w_a = jnp.concatenate([w_in[:, :, :o3], w_in[:, :, o2:o3]], axis=-1).astype(BF16)
    w_u = w_in[:, :, o3:o4].astype(BF16)
    w_gate = w_in[:, :, o4:].astype(BF16)
    qb = w_qb.reshape(n_layers, Q_LORA_RANK, N_HEADS, QK_NOPE_DIM + QK_ROPE_DIM)
    w_qb2 = jnp.concatenate([qb[..., :QK_NOPE_DIM].reshape(n_layers, Q_LORA_RANK, -1),
                             qb[..., QK_NOPE_DIM:].reshape(n_layers, Q_LORA_RANK, -1)], axis=-1).astype(BF16)
    kvb = w_kvb.reshape(n_layers, KV_LORA_RANK, N_HEADS, QK_NOPE_DIM + V_HEAD_DIM)
    w_k = kvb[..., :QK_NOPE_DIM].reshape(n_layers, KV_LORA_RANK, -1).astype(BF16)
    w_vt = jnp.swapaxes(kvb[..., QK_NOPE_DIM:].reshape(n_layers, KV_LORA_RANK, -1), 1, 2).astype(BF16)
    return dict(w_a=w_a, w_u=w_u, w_gate=w_gate, w_qb=w_qb2, w_k=w_k, w_vt=w_vt,
                w_pool=w_pool.astype(BF16), w_out=w_out.astype(BF16),
                w1_gu=w1_gu.astype(BF16), w1_down=w1_down.astype(BF16),
                w2_gu=w2_gu.astype(BF16), w2_down=w2_down.astype(BF16))


def kernel(x_prompt, x_sample, c_prompt, c_sample, w_ada, b_ada, n1_pre, w1_gu, w1_down, n1_post, nm_pre, w_in, q_a_norm, w_qb, kv_a_norm, w_kvb, w_pool, pool_scale, w_out, nm_post, n2_pre, w2_gu, w2_down, n2_post):
    n_layers, d = n1_pre.shape
    seq_len = x_prompt.shape[1]
    assert x_sample.shape[1] == seq_len and seq_len % TOKEN_TILE == 0 and seq_len % KV_CHUNK == 0
    assert TOKEN_TILE % POOL_HALO == 0 and seq_len % Q_TILE == 0
    assert seq_len % FFN_TILE == 0 and FFN_TILE % (FFN_SUBTILES * SUBLANES) == 0
    nb_p, nb_s = x_prompt.shape[0], x_sample.shape[0]
    n_seq = nb_p + nb_s

    x = jnp.concatenate([x_prompt.reshape(-1, d), x_sample.reshape(-1, d)], axis=0)
    c = jnp.concatenate([c_prompt, c_sample], axis=0)
    bp = -(-n_seq // SUBLANES) * SUBLANES
    c = jnp.pad(c, ((0, bp - n_seq), (0, 0)))
    mod = _ada_modulation(c, w_ada, b_ada)
    mod = jnp.swapaxes(mod, 1, 2)

    w = _prepare_weights(w_in, w_qb, w_kvb, w_pool, w_out, w1_gu, w1_down, w2_gu, w2_down)
    cos2, sin2 = _rope_tables(seq_len)
    row = lambda a, l: a[l][None, :]

    for l in range(n_layers):
        x = _ffn(x, mod[l], row(n1_pre, l), w["w1_gu"][l], w["w1_down"][l], row(n1_post, l),
                 row0=0, seq_len=seq_len)
        q, k, v_t = _qkv(x, mod[l], row(nm_pre, l), w["w_a"][l], row(q_a_norm, l), w["w_qb"][l],
                         row(kv_a_norm, l), w["w_k"][l], w["w_vt"][l], cos2, sin2,
                         n_seq=n_seq, seq_len=seq_len)
        o_attn = _attention(q, k, v_t).reshape(n_seq * seq_len, -1)
        x = _post(x, o_attn, mod[l], row(nm_pre, l), w["w_u"][l], w["w_gate"][l], w["w_pool"][l],
                  row(pool_scale, l), w["w_out"][l], row(nm_post, l), seq_len=seq_len)
        x = _ffn(x, mod[l], row(n2_pre, l), w["w2_gu"][l], w["w2_down"][l], row(n2_post, l),
                 row0=6, seq_len=seq_len)

    x = x.reshape(n_seq, seq_len, d)
    return (x[:nb_p], x[nb_p:])
```
